```python
import math
import jax, jax.numpy as jnp
from jax import lax
import numpy as np

D_MODEL = 1024
BATCH = 8
SEQ = 2048
DEPTH = 1
DEC_BATCH = 128
DEC_SEQ = 4
PAST_LEN = 16384
PAGE_SIZE = 128

D_MIX = D_MODEL
D_SSM = D_MIX // 2
SSM_GROUP = 16
N_SSM_GROUPS = D_SSM // SSM_GROUP
SSM_STATE = 64
D_GLA = D_MIX - D_SSM
GLA_HEADS = 4
GLA_DV = D_GLA // GLA_HEADS
GLA_DK = GLA_DV // 2
GLA_GATE_RANK = 16
GLA_TAU = 16.0
GLA_CHUNK = 64
PEER_KEYS = 128
PEER_EXPERTS = PEER_KEYS * PEER_KEYS
PEER_HEADS = 8
PEER_QDIM = 256
PEER_TOPK = 16
PEER_TOKEN_BLOCK = 128
PLE_DIM = 256
EPS = 1e-6

IN_SPLITS = (D_SSM, GLA_HEADS * GLA_DK, GLA_HEADS * GLA_DK, D_GLA, GLA_GATE_RANK, D_GLA)
N_IN = sum(IN_SPLITS)
IN_OFFSETS = tuple(int(v) for v in np.cumsum(IN_SPLITS)[:-1])

kernel_name = "hymba_s5_gla_peer_step"


def _rmsnorm(x, g):
    xf = x.astype(jnp.float32)
    y = xf * lax.rsqrt(jnp.mean(xf * xf, axis=-1, keepdims=True) + EPS) * g.astype(jnp.float32)
    return y.astype(x.dtype)


def _cplx_affine_combine(e1, e2):
    a1r, a1i, b1r, b1i = e1
    a2r, a2i, b2r, b2i = e2
    ar = a2r * a1r - a2i * a1i
    ai = a2r * a1i + a2i * a1r
    br = a2r * b1r - a2i * b1i + b2r
    bi = a2r * b1i + a2i * b1r + b2i
    return ar, ai, br, bi


def _s5_mixer(u, h0_re, h0_im, lam_re, lam_im, log_step, b_re, b_im, c_re, c_im,
              d_skip, w_glu, b_glu, g_out):
    bsz, L, _ = u.shape
    f32 = jnp.float32
    uf = u.astype(f32).reshape(bsz, L, N_SSM_GROUPS, SSM_GROUP)
    lr = lam_re.astype(f32)
    li = lam_im.astype(f32)
    step = jnp.exp(log_step.astype(f32))[:, None]
    mag = jnp.exp(lr * step)
    ab_re = mag * jnp.cos(li * step)
    ab_im = mag * jnp.sin(li * step)
    den = lr * lr + li * li
    nr = ab_re - 1.0
    ni = ab_im
    coef_re = (nr * lr + ni * li) / den
    coef_im = (ni * lr - nr * li) / den
    bu_re = jnp.einsum('blgs,gps->blgp', uf, b_re.astype(f32))
    bu_im = jnp.einsum('blgs,gps->blgp', uf, b_im.astype(f32))
    br = coef_re * bu_re - coef_im * bu_im
    bi = coef_re * bu_im + coef_im * bu_re
    h0r = h0_re.astype(f32)
    h0i = h0_im.astype(f32)
    br = br.at[:, 0].add(ab_re * h0r - ab_im * h0i)
    bi = bi.at[:, 0].add(ab_re * h0i + ab_im * h0r)
    ar = jnp.broadcast_to(ab_re, br.shape)
    ai = jnp.broadcast_to(ab_im, bi.shape)
    _, _, h_re, h_im = lax.associative_scan(_cplx_affine_combine, (ar, ai, br, bi), axis=1)
    y = (jnp.einsum('gsp,blgp->blgs', c_re.astype(f32), h_re)
         - jnp.einsum('gsp,blgp->blgs', c_im.astype(f32), h_im))
    y = y.reshape(bsz, L, D_SSM) + d_skip.astype(f32) * uf.reshape(bsz, L, D_SSM)
    yg = jax.nn.gelu(y)
    out = yg * jax.nn.sigmoid(yg @ w_glu.astype(f32) + b_glu.astype(f32))
    out = _rmsnorm(out, g_out)
    return out, h_re[:, -1], h_im[:, -1]


def _gla_mixer(q, k, v, gk_low, og, s0, w_gk2, b_gk, g_head):
    bsz, L, _ = q.shape
    f32 = jnp.float32
    c = math.gcd(L, GLA_CHUNK)
    n = L // c

    def heads(t, d):
        return t.astype(f32).reshape(bsz, n, c, GLA_HEADS, d).transpose(1, 0, 3, 2, 4)

    qh = heads(q, GLA_DK) * (GLA_DK ** -0.5)
    kh = heads(k, GLA_DK)
    vh = heads(v, GLA_DV)
    logg = jax.nn.log_sigmoid(gk_low.astype(f32) @ w_gk2.astype(f32) + b_gk.astype(f32)) / GLA_TAU
    gh = heads(logg, GLA_DK)
    causal = jnp.tril(jnp.ones((c, c), dtype=bool))

    def chunk_step(S, inp):
        qc, kc, vc, gc = inp
        bcum = jnp.cumsum(gc, axis=2)
        o_inter = jnp.einsum('bhtk,bhkv->bhtv', qc * jnp.exp(bcum), S)
        diff = bcum[:, :, :, None, :] - bcum[:, :, None, :, :]
        decay = jnp.exp(jnp.where(causal[:, :, None], diff, -jnp.inf))
        scores = jnp.einsum('bhtk,bhsk,bhtsk->bhts', qc, kc, decay)
        o = o_inter + jnp.einsum('bhts,bhsv->bhtv', scores, vc)
        blast = bcum[:, :, -1:, :]
        S_new = (jnp.exp(blast[:, :, 0, :])[..., None] * S
                 + jnp.einsum('bhsk,bhsv->bhkv', kc * jnp.exp(blast - bcum), vc))
        return S_new, o

    S_fin, o = lax.scan(chunk_step, s0.astype(f32), (qh, kh, vh, gh))
    o = o.transpose(1, 0, 3, 2, 4).reshape(bsz, L, GLA_HEADS, GLA_DV)
    o = _rmsnorm(o, g_head).reshape(bsz, L, D_GLA)
    o = o * jax.nn.silu(og.astype(f32))
    return o, S_fin


def _peer_ffn(xn, w_query, sub_keys1, sub_keys2, peer_u, peer_v):
    bsz, L, d = xn.shape
    f32 = jnp.float32
    T = bsz * L
    xt = xn.reshape(T, d)
    q = (xt @ w_query).astype(f32).reshape(T, PEER_HEADS, 2, PEER_QDIM // 2)
    s1 = jnp.einsum('thd,hnd->thn', q[:, :, 0], sub_keys1.astype(f32))
    s2 = jnp.einsum('thd,hnd->thn', q[:, :, 1], sub_keys2.astype(f32))
    v1, i1 = lax.top_k(s1, PEER_TOPK)
    v2, i2 = lax.top_k(s2, PEER_TOPK)
    cand = (v1[..., :, None] + v2[..., None, :]).reshape(T, PEER_HEADS, PEER_TOPK * PEER_TOPK)
    best, flat = lax.top_k(cand, PEER_TOPK)
    e1 = jnp.take_along_axis(i1, flat // PEER_TOPK, axis=-1)
    e2 = jnp.take_along_axis(i2, flat % PEER_TOPK, axis=-1)
    experts = (e1 * PEER_KEYS + e2).reshape(T, PEER_HEADS * PEER_TOPK)
    gates = jax.nn.softmax(best, axis=-1).reshape(T, PEER_HEADS * PEER_TOPK)
    blk = min(PEER_TOKEN_BLOCK, T)
    nblk = -(-T // blk)
    pad = nblk * blk - T
    xt_p = jnp.pad(xt, ((0, pad), (0, 0))).reshape(nblk, blk, d)
    ex_p = jnp.pad(experts, ((0, pad), (0, 0))).reshape(nblk, blk, -1)
    gt_p = jnp.pad(gates, ((0, pad), (0, 0))).reshape(nblk, blk, -1)

    def block(args):
        xb, eb, gb = args
        u = peer_u[eb]
        vv = peer_v[eb]
        act = jax.nn.gelu(jnp.einsum('td,tkd->tk', xb, u).astype(f32))
        return jnp.einsum('tk,tkd->td', (gb * act).astype(vv.dtype), vv)

    out = lax.map(block, (xt_p, ex_p, gt_p))
    return out.reshape(nblk * blk, d)[:T].reshape(bsz, L, d).astype(xn.dtype)


def _layer(x, pe, h0_re, h0_im, s0_gla,
           g_mix, w_in, w_out, lam_re, lam_im, log_step, b_re, b_im, c_re, c_im,
           d_skip, w_glu, b_glu, g_ssm_out, w_gk2, b_gk, g_gla_head,
           g_ffn, w_query, sub_keys1, sub_keys2, peer_u, peer_v,
           g_ple, w_ple, w_ple_gate, b_ple_gate):
    h = _rmsnorm(x, g_mix)
    z = h @ w_in
    u_ssm, q, k, v, gk_low, og = jnp.split(z, IN_OFFSETS, axis=-1)
    ssm_out, h_re, h_im = _s5_mixer(u_ssm, h0_re, h0_im, lam_re, lam_im, log_step, b_re, b_im,
                                    c_re, c_im, d_skip, w_glu, b_glu, g_ssm_out)
    gla_out, s_gla = _gla_mixer(q, k, v, gk_low, og, s0_gla, w_gk2, b_gk, g_gla_head)
    mix = jnp.concatenate([ssm_out.astype(jnp.float32), gla_out], axis=-1).astype(x.dtype)
    x = x + (mix @ w_out).astype(x.dtype)
    x = x + _peer_ffn(_rmsnorm(x, g_ffn), w_query, sub_keys1, sub_keys2, peer_u, peer_v)
    gate = jax.nn.sigmoid((_rmsnorm(x, g_ple) @ w_ple_gate + b_ple_gate).astype(jnp.float32))
    x = x + (gate * (pe @ w_ple).astype(jnp.float32)).astype(x.dtype)
    return x, h_re, h_im, s_gla


def setup_inputs(seed: int = 0) -> dict:
    key = jax.random.key(seed)
    ks = iter(jax.random.split(key, 48))
    f32 = jnp.float32

    def nrm(shape, scale):
        return jax.random.normal(next(ks), shape, f32) * scale

    def gain(shape):
        return 1.0 + 0.02 * jax.random.normal(next(ks), shape, f32)

    G, P = N_SSM_GROUPS, SSM_STATE
    inp = {}
    inp['x_prompt'] = nrm((BATCH, SEQ, D_MODEL), 1.0)
    inp['x_sample'] = nrm((DEC_BATCH, DEC_SEQ, D_MODEL), 1.0)
    inp['state_ssm_re'] = nrm((DEPTH, DEC_BATCH, G, P), 0.5)
    inp['state_ssm_im'] = nrm((DEPTH, DEC_BATCH, G, P), 0.5)
    inp['state_gla'] = nrm((DEPTH, DEC_BATCH, GLA_HEADS, GLA_DK, GLA_DV), 1.0)
    inp['p_prompt'] = nrm((DEPTH, BATCH, SEQ, PLE_DIM), 1.0)
    inp['p_sample'] = nrm((DEPTH, DEC_BATCH, DEC_SEQ, PLE_DIM), 1.0)
    inp['g_mix'] = gain((DEPTH, D_MODEL))
    inp['w_in'] = nrm((DEPTH, D_MODEL, N_IN), D_MODEL ** -0.5)
    inp['w_out'] = nrm((DEPTH, D_MIX, D_MODEL), D_MIX ** -0.5)
    inp['lam_re'] = -0.5 + nrm((DEPTH, G, P), 0.01)
    inp['lam_im'] = (math.pi * jnp.arange(P, dtype=f32))[None, None, :] + nrm((DEPTH, G, P), 0.01)
    inp['log_step'] = jax.random.uniform(next(ks), (DEPTH, G), f32, math.log(1e-3), math.log(1e-1))
    inp['b_re'] = nrm((DEPTH, G, P, SSM_GROUP), (2 * SSM_GROUP) ** -0.5)
    inp['b_im'] = nrm((DEPTH, G, P, SSM_GROUP), (2 * SSM_GROUP) ** -0.5)
    inp['c_re'] = nrm((DEPTH, G, SSM_GROUP, P), (2 * P) ** -0.5)
    inp['c_im'] = nrm((DEPTH, G, SSM_GROUP, P), (2 * P) ** -0.5)
    inp['d_skip'] = nrm((DEPTH, D_SSM), 1.0)
    inp['w_glu'] = nrm((DEPTH, D_SSM, D_SSM), D_SSM ** -0.5)
    inp['b_glu'] = nrm((DEPTH, D_SSM), 0.02)
    inp['g_ssm_out'] = gain((DEPTH, D_SSM))
    inp['w_gk2'] = nrm((DEPTH, GLA_GATE_RANK, GLA_HEADS * GLA_DK), GLA_GATE_RANK ** -0.5)
    inp['b_gk'] = nrm((DEPTH, GLA_HEADS * GLA_DK), 0.02)
    inp['g_gla_head'] = gain((DEPTH, GLA_DV))
    inp['g_ffn'] = gain((DEPTH, D_MODEL))
    inp['w_query'] = nrm((DEPTH, D_MODEL, PEER_HEADS * PEER_QDIM), D_MODEL ** -0.5)
    inp['sub_keys1'] = nrm((DEPTH, PEER_HEADS, PEER_KEYS, PEER_QDIM // 2), (PEER_QDIM // 2) ** -0.5)
    inp['sub_keys2'] = nrm((DEPTH, PEER_HEADS, PEER_KEYS, PEER_QDIM // 2), (PEER_QDIM // 2) ** -0.5)
    inp['peer_u'] = nrm((DEPTH, PEER_EXPERTS, D_MODEL), D_MODEL ** -0.5)
    inp['peer_v'] = nrm((DEPTH, PEER_EXPERTS, D_MODEL), PEER_HEADS ** -0.5)
    inp['g_ple'] = gain((DEPTH, D_MODEL))
    inp['w_ple'] = nrm((DEPTH, PLE_DIM, D_MODEL), PLE_DIM ** -0.5)
    inp['w_ple_gate'] = nrm((DEPTH, D_MODEL, D_MODEL), D_MODEL ** -0.5)
    inp['b_ple_gate'] = nrm((DEPTH, D_MODEL), 0.02)
    inp['g_final'] = gain((D_MODEL,))
    return inp


def reference(x_prompt, x_sample, state_ssm_re, state_ssm_im, state_gla, p_prompt, p_sample,
              g_mix, w_in, w_out, lam_re, lam_im, log_step, b_re, b_im, c_re, c_im,
              d_skip, w_glu, b_glu, g_ssm_out, w_gk2, b_gk, g_gla_head,
              g_ffn, w_query, sub_keys1, sub_keys2, peer_u, peer_v,
              g_ple, w_ple, w_ple_gate, b_ple_gate, g_final):
    f32 = jnp.float32
    bp = x_prompt.shape[0]
    xp = x_prompt
    xs = x_sample
    pr_re, pr_im, pr_gla = [], [], []
    sm_re, sm_im, sm_gla = [], [], []
    for i in range(DEPTH):
        wl = (g_mix[i], w_in[i], w_out[i], lam_re[i], lam_im[i], log_step[i], b_re[i], b_im[i],
              c_re[i], c_im[i], d_skip[i], w_glu[i], b_glu[i], g_ssm_out[i], w_gk2[i], b_gk[i],
              g_gla_head[i], g_ffn[i], w_query[i], sub_keys1[i], sub_keys2[i], peer_u[i], peer_v[i],
              g_ple[i], w_ple[i], w_ple_gate[i], b_ple_gate[i])
        z_ssm = jnp.zeros((bp, N_SSM_GROUPS, SSM_STATE), f32)
        z_gla = jnp.zeros((bp, GLA_HEADS, GLA_DK, GLA_DV), f32)
        xp, hr, hi, sg = _layer(xp, p_prompt[i], z_ssm, z_ssm, z_gla, *wl)
        pr_re.append(hr)
        pr_im.append(hi)
        pr_gla.append(sg)
        xs, hr, hi, sg = _layer(xs, p_sample[i], state_ssm_re[i], state_ssm_im[i], state_gla[i], *wl)
        sm_re.append(hr)
        sm_im.append(hi)
        sm_gla.append(sg)
    y_prompt = _rmsnorm(xp, g_final)
    y_sample = _rmsnorm(xs, g_final)
    return (y_prompt, y_sample,
            jnp.stack(pr_re), jnp.stack(pr_im), jnp.stack(pr_gla),
            jnp.stack(sm_re), jnp.stack(sm_im), jnp.stack(sm_gla))
```

```python
import functools
import math

import jax
import jax.numpy as jnp
from jax import lax
from jax.experimental import pallas as pl
from jax.experimental.pallas import tpu as pltpu

D_MODEL = 1024
D_SSM = 512
SSM_GROUP = 16
N_GROUPS = 32
SSM_STATE = 64
SSM_LANES = N_GROUPS * SSM_STATE
D_GLA = 512
GLA_HEADS = 4
GLA_DV = 128
GLA_DK = 64
GLA_QK = GLA_HEADS * GLA_DK
GLA_RANK = 16
GLA_TAU = 16.0
GLA_CHUNK = 64
GLA_SUB = 16
PEER_KEYS = 128
PEER_HEADS = 8
PEER_HALF = 128
PEER_TOPK = 16
PLE_DIM = 256
EPS = 1e-6
LANE = 128
Z_COLS = D_SSM + 2 * GLA_QK + 2 * D_GLA + LANE

F32 = jnp.float32
BF16 = jnp.bfloat16
HIGHEST = lax.Precision.HIGHEST


def _rms(x, g):
    return x * lax.rsqrt(jnp.mean(x * x, axis=-1, keepdims=True) + EPS) * g


def _mm(a, b, exact):
    if exact:
        return jnp.dot(a, b, preferred_element_type=F32, precision=HIGHEST)
    return jnp.dot(a.astype(BF16), b.astype(BF16), preferred_element_type=F32)


def _mm_nt(a, b, exact):
    dn = (((1,), (1,)), ((), ()))
    if exact:
        return lax.dot_general(a, b, dn, preferred_element_type=F32, precision=HIGHEST)
    return lax.dot_general(a.astype(BF16), b.astype(BF16), dn, preferred_element_type=F32)


def _mm_tn(a, b, exact):
    dn = (((0,), (0,)), ((), ()))
    if exact:
        return lax.dot_general(a, b, dn, preferred_element_type=F32, precision=HIGHEST)
    return lax.dot_general(a.astype(BF16), b.astype(BF16), dn, preferred_element_type=F32)


def _params(*sem):
    return pltpu.CompilerParams(dimension_semantics=sem)


def _full(shape):
    return pl.BlockSpec(shape, lambda *_: (0,) * len(shape))


def _s5_prep_kernel(lr_ref, li_ref, ls_ref, bre_ref, bim_ref, abr_ref, abi_ref, pre_ref, pim_ref):
    lr = lr_ref[...]
    li = li_ref[...]
    step = jnp.exp(ls_ref[...])
    mag = jnp.exp(lr * step)
    ab_re = mag * jnp.cos(li * step)
    ab_im = mag * jnp.sin(li * step)
    den = lr * lr + li * li
    nr = ab_re - 1.0
    ni = ab_im
    coef_re = (nr * lr + ni * li) / den
    coef_im = (ni * lr - nr * li) / den
    abr_ref[...] = ab_re
    abi_ref[...] = ab_im
    for s in range(SSM_GROUP):
        bre = bre_ref[s]
        bim = bim_ref[s]
        pre_ref[s] = coef_re * bre - coef_im * bim
        pim_ref[s] = coef_re * bim + coef_im * bre


def _s5_prep(lam_re, lam_im, log_step, b_re, b_im):
    G, P, S = N_GROUPS, SSM_STATE, SSM_GROUP
    ab_re, ab_im, p_re, p_im = pl.pallas_call(
        _s5_prep_kernel,
        out_shape=(jax.ShapeDtypeStruct((G, P), F32), jax.ShapeDtypeStruct((G, P), F32),
                   jax.ShapeDtypeStruct((S, G, P), F32), jax.ShapeDtypeStruct((S, G, P), F32)),
        name="s5_prep",
    )(lam_re, lam_im, log_step.reshape(G, 1), b_re.transpose(2, 0, 1), b_im.transpose(2, 0, 1))
    return ab_re, ab_im, p_re.transpose(1, 0, 2), p_im.transpose(1, 0, 2)


def _block_diag(w):
    G, a, b = w.shape
    eye = jnp.eye(G, dtype=w.dtype)
    return (eye[:, None, :, None] * w[:, :, None, :]).reshape(G * a, G * b)


def _inproj_kernel(x_ref, g_ref, w_ref, wgk_ref, bgk_ref,
                   u_ref, q_ref, k_ref, v_ref, og_ref, lg_ref, *, exact):
    h = _rms(x_ref[...], g_ref[...])
    z = _mm(h, w_ref[...], exact)
    o = 0
    u_ref[...] = z[:, o:o + D_SSM]; o += D_SSM
    q_ref[...] = z[:, o:o + GLA_QK]; o += GLA_QK
    k_ref[...] = z[:, o:o + GLA_QK]; o += GLA_QK
    v_ref[...] = z[:, o:o + D_GLA]; o += D_GLA
    og_ref[...] = z[:, o:o + D_GLA]; o += D_GLA
    pre = _mm(z[:, o:o + LANE], wgk_ref[...], exact) + bgk_ref[...]
    lg_ref[...] = (jnp.minimum(pre, 0.0) - jnp.log(1.0 + jnp.exp(-jnp.abs(pre)))) * (1.0 / GLA_TAU)


def _inproj(x, g_mix, w_pad, wgk_pad, b_gk, *, tm, exact):
    T = x.shape[0]
    row = lambda n: pl.BlockSpec((tm, n), lambda i: (i, 0))
    outs = (D_SSM, GLA_QK, GLA_QK, D_GLA, D_GLA, GLA_QK)
    return pl.pallas_call(
        functools.partial(_inproj_kernel, exact=exact),
        grid=(T // tm,),
        in_specs=[row(D_MODEL), _full((1, D_MODEL)), _full((D_MODEL, Z_COLS)),
                  _full((LANE, GLA_QK)), _full((1, GLA_QK))],
        out_specs=[row(n) for n in outs],
        out_shape=[jax.ShapeDtypeStruct((T, n), F32) for n in outs],
        compiler_params=_params("parallel"),
        name="inproj_exact" if exact else "inproj",
    )(x, g_mix.reshape(1, -1), w_pad, wgk_pad, b_gk.reshape(1, -1))


def _s5_kernel(u_ref, h0r_ref, h0i_ref, abr_ref, abi_ref, bm_ref, cre_ref, cim_ref,
               dsk_ref, wglu_ref, bglu_ref, gout_ref,
               y_ref, hr_out, hi_out, x_s, hr_s, hi_s, *, bb, tl, cw, exact):
    rows = bb * tl

    @pl.when(pl.program_id(1) == 0)
    def _():
        hr_s[...] = h0r_ref[...]
        hi_s[...] = h0i_ref[...]

    u = u_ref[...].reshape(rows, D_SSM)
    nt = SSM_LANES // LANE
    for half in range(2):
        xh = _mm(u, bm_ref[:, half * SSM_LANES:(half + 1) * SSM_LANES], exact)
        for j in range(nt):
            x_s[half * nt + j] = xh[:, j * LANE:(j + 1) * LANE]

    tpc = cw // LANE
    for c in range(nt // tpc):
        tiles = range(c * tpc, (c + 1) * tpc)
        ar = [jnp.broadcast_to(abr_ref[:, j * LANE:(j + 1) * LANE], (bb, LANE)) for j in tiles]
        ai = [jnp.broadcast_to(abi_ref[:, j * LANE:(j + 1) * LANE], (bb, LANE)) for j in tiles]

        def step(t, carry):
            rsl = pl.ds(t, bb, stride=tl)
            out = []
            for n, j in enumerate(tiles):
                hr, hi = carry[n]
                nhr = ar[n] * hr - ai[n] * hi + x_s[j, rsl, :]
                nhi = ar[n] * hi + ai[n] * hr + x_s[nt + j, rsl, :]
                x_s[j, rsl, :] = nhr
                x_s[nt + j, rsl, :] = nhi
                out.append((nhr, nhi))
            return tuple(out)

        init = tuple((hr_s[:, j * LANE:(j + 1) * LANE], hi_s[:, j * LANE:(j + 1) * LANE]) for j in tiles)
        fin = lax.fori_loop(0, tl, step, init, unroll=min(tl, 8))
        for n, j in enumerate(tiles):
            hr_s[:, j * LANE:(j + 1) * LANE] = fin[n][0]
            hi_s[:, j * LANE:(j + 1) * LANE] = fin[n][1]

    h_re = jnp.concatenate([x_s[j] for j in range(nt)], axis=1)
    h_im = jnp.concatenate([x_s[nt + j] for j in range(nt)], axis=1)
    y = (_mm(h_re, cre_ref[...], False) - _mm(h_im, cim_ref[...], False) + dsk_ref[...] * u)
    yg = jax.nn.gelu(y)
    out = yg * jax.nn.sigmoid(_mm(yg, wglu_ref[...], False) + bglu_ref[...])
    y_ref[...] = _rms(out, gout_ref[...]).reshape(y_ref.shape)

    @pl.when(pl.program_id(1) == pl.num_programs(1) - 1)
    def _():
        hr_out[...] = hr_s[...]
        hi_out[...] = hi_s[...]


def _s5(u, h0r, h0i, ab_re, ab_im, bmat, cre, cim, d_skip, w_glu, b_glu, g_out,
        *, bb, tl, cw, exact):
    if u.ndim == 3:
        B, L, _ = u.shape
        u_spec = pl.BlockSpec((bb, tl, D_SSM), lambda i, j: (i, j, 0))
        grid = (B // bb, L // tl)
    else:
        B = u.shape[0] // tl
        u_spec = pl.BlockSpec((bb * tl, D_SSM), lambda i, j: (i, 0))
        grid = (B // bb, 1)
    st_spec = pl.BlockSpec((bb, SSM_LANES), lambda i, j: (i, 0))
    wdt = F32 if exact else BF16
    return pl.pallas_call(
        functools.partial(_s5_kernel, bb=bb, tl=tl, cw=cw, exact=exact),
        grid=grid,
        in_specs=[u_spec, st_spec, st_spec, _full((1, SSM_LANES)), _full((1, SSM_LANES)),
                  _full((D_SSM, 2 * SSM_LANES)), _full((SSM_LANES, D_SSM)), _full((SSM_LANES, D_SSM)),
                  _full((1, D_SSM)), _full((D_SSM, D_SSM)), _full((1, D_SSM)), _full((1, D_SSM))],
        out_specs=[u_spec, st_spec, st_spec],
        out_shape=[jax.ShapeDtypeStruct(u.shape, F32),
                   jax.ShapeDtypeStruct((B, SSM_LANES), F32), jax.ShapeDtypeStruct((B, SSM_LANES), F32)],
        scratch_shapes=[pltpu.VMEM((2 * SSM_LANES // LANE, bb * tl, LANE), F32),
                        pltpu.VMEM((bb, SSM_LANES), F32), pltpu.VMEM((bb, SSM_LANES), F32)],
        compiler_params=_params("parallel", "arbitrary"),
        name="s5_exact" if exact else "s5",
    )(u, h0r, h0i, ab_re.reshape(1, -1), ab_im.reshape(1, -1), bmat.astype(wdt), cre.astype(BF16),
      cim.astype(BF16), d_skip.reshape(1, -1), w_glu.astype(BF16), b_glu.reshape(1, -1), g_out.reshape(1, -1))


def _gla_chunk(q, k, v, g, st_ref, *, c, sub):
    ri = lax.broadcasted_iota(jnp.int32, (c, c), 0)
    ci = lax.broadcasted_iota(jnp.int32, (c, c), 1)
    tri = (ci <= ri).astype(F32)
    bcum = jnp.dot(tri, g, preferred_element_type=F32, precision=HIGHEST)
    outs = []
    for h in range(GLA_HEADS):
        ks = slice(h * GLA_DK, (h + 1) * GLA_DK)
        qh = q[:, ks] * (GLA_DK ** -0.5)
        kh = k[:, ks]
        bh = bcum[:, ks]
        vh = v[:, h * GLA_DV:(h + 1) * GLA_DV]
        st = st_ref[h]
        o_inter = _mm_nt(qh * jnp.exp(bh), st, False)
        parts = []
        for i in range(c // sub):
            r0 = i * sub
            n = r0 + sub
            ref_pt = bh[r0 - 1:r0, :] if i > 0 else jnp.zeros((1, GLA_DK), F32)
            qi = qh[r0:n] * jnp.exp(bh[r0:n] - ref_pt)
            ki = kh[:n] * jnp.exp(ref_pt - bh[:n])
            s = _mm_nt(qi, ki, False)
            rr = lax.broadcasted_iota(jnp.int32, (sub, n), 0) + r0
            cc = lax.broadcasted_iota(jnp.int32, (sub, n), 1)
            s = jnp.where(cc <= rr, s, 0.0)
            parts.append(_mm(s, vh[:n], False))
        o = o_inter + (jnp.concatenate(parts, axis=0) if len(parts) > 1 else parts[0])
        blast = bh[c - 1:c, :]
        kt = kh * jnp.exp(blast - bh)
        st_ref[h] = st * jnp.exp(blast) + _mm_tn(vh, kt, False)
        outs.append(o)
    return outs


def _gla_kernel(q_ref, k_ref, v_ref, g_ref, og_ref, s0_ref, gh_ref, o_ref, sout_ref, st_s,
                *, nb, nchunk, c, sub):
    def run_batch(bi):
        if nb > 1:
            st_s[...] = s0_ref[bi]
        for j in range(nchunk):
            r = pl.ds((bi * nchunk + j) * c, c)
            outs = _gla_chunk(q_ref[r, :], k_ref[r, :], v_ref[r, :], g_ref[r, :], st_s, c=c, sub=sub)
            gh = gh_ref[...]
            og = og_ref[r, :]
            for h in range(GLA_HEADS):
                vs = slice(h * GLA_DV, (h + 1) * GLA_DV)
                o_ref[r, vs] = _rms(outs[h], gh) * jax.nn.silu(og[:, vs])
        if nb > 1:
            sout_ref[bi] = st_s[...]

    if nb > 1:
        for bi in range(nb):
            run_batch(bi)
    else:
        @pl.when(pl.program_id(1) == 0)
        def _():
            st_s[...] = s0_ref[0]

        run_batch(0)

        @pl.when(pl.program_id(1) == pl.num_programs(1) - 1)
        def _():
            sout_ref[0] = st_s[...]


def _gla(q, k, v, g, og, s0t, g_head, *, B, L, nb, nchunk, c, sub):
    rows = nb * nchunk * c
    nj = L // (nchunk * c)
    tok = lambda n: pl.BlockSpec((rows, n), lambda i, j: (i * nj + j, 0))
    st_spec = pl.BlockSpec((nb, GLA_HEADS, GLA_DV, GLA_DK), lambda i, j: (i, 0, 0, 0))
    return pl.pallas_call(
        functools.partial(_gla_kernel, nb=nb, nchunk=nchunk, c=c, sub=sub),
        grid=(B // nb, nj),
        in_specs=[tok(GLA_QK), tok(GLA_QK), tok(D_GLA), tok(GLA_QK), tok(D_GLA), st_spec, _full((1, GLA_DV))],
        out_specs=[tok(D_GLA), st_spec],
        out_shape=[jax.ShapeDtypeStruct((B * L, D_GLA), F32),
                   jax.ShapeDtypeStruct((B, GLA_HEADS, GLA_DV, GLA_DK), F32)],
        scratch_shapes=[pltpu.VMEM((GLA_HEADS, GLA_DV, GLA_DK), F32)],
        compiler_params=_params("parallel", "arbitrary"),
        name="gla",
    )(q, k, v, g, og, s0t, g_head.reshape(1, -1))


def _outproj_kernel(x_ref, a_ref, b_ref, wa_ref, wb_ref, o_ref):
    o_ref[...] = x_ref[...] + (_mm(a_ref[...], wa_ref[...], False) + _mm(b_ref[...], wb_ref[...], False))


def _outproj(x, ssm_out, gla_out, w_out, *, tm):
    T = x.shape[0]
    row = lambda n: pl.BlockSpec((tm, n), lambda i: (i, 0))
    w = w_out.astype(BF16)
    return pl.pallas_call(
        _outproj_kernel,
        grid=(T // tm,),
        in_specs=[row(D_MODEL), row(D_SSM), row(D_GLA), _full((D_SSM, D_MODEL)), _full((D_GLA, D_MODEL))],
        out_specs=row(D_MODEL),
        out_shape=jax.ShapeDtypeStruct((T, D_MODEL), F32),
        compiler_params=_params("parallel"),
        name="outproj",
    )(x, ssm_out, gla_out, w[:D_SSM], w[D_SSM:])


PEER_EB = 8
NEG = float("-inf")


def _top_distinct(work, mult, n):
    vals, cnts = [], []
    for _ in range(n):
        m = jnp.max(work, axis=0, keepdims=True)
        eq = work == m
        vals.append(m)
        cnts.append(jnp.sum(jnp.where(eq, mult, 0.0), axis=0, keepdims=True))
        work = jnp.where(eq, NEG, work)
    return vals, cnts


def _peer_kernel(x_ref, gffn_ref, wq_ref, k1_ref, k2_ref, u_ref, vt_ref, o_ref,
                 xnt_s, qt_s, s2_s, p_s, c_s, th_s, acc_s, w_s, *, tm):
    eb = pl.program_id(1)
    nlt = tm // LANE

    @pl.when(eb == 0)
    def _prologue():
        xn = _rms(x_ref[...], gffn_ref[...])
        xnt = xn.T.astype(BF16)
        xnt_s[...] = xnt
        qt_s[...] = jnp.dot(wq_ref[...], xnt, preferred_element_type=F32)
        acc_s[...] = jnp.zeros_like(acc_s)

        def head_body(h, _):
            q1 = qt_s[pl.ds(pl.multiple_of(h * 2 * PEER_HALF, LANE), PEER_HALF), :]
            q2 = qt_s[pl.ds(pl.multiple_of(h * 2 * PEER_HALF + PEER_HALF, LANE), PEER_HALF), :]
            s1 = _mm(k1_ref[h], q1, False)
            s2 = _mm(k2_ref[h], q2, False)
            s2_s[h] = s2
            for lt in range(nlt):
                ls = slice(lt * LANE, (lt + 1) * LANE)
                s1t = s1[:, ls]
                s2t = s2[:, ls]
                v1, n1 = _top_distinct(s1t, 1.0, PEER_TOPK + 1)
                v2, n2 = _top_distinct(s2t, 1.0, PEER_TOPK + 1)
                v2m = jnp.concatenate(v2[:PEER_TOPK], axis=0)
                n2m = jnp.concatenate(n2[:PEER_TOPK], axis=0)
                cand = jnp.concatenate([v1[a] + v2m for a in range(PEER_TOPK)], axis=0)
                mult = jnp.concatenate([n1[a] * n2m for a in range(PEER_TOPK)], axis=0)
                top, ntop = _top_distinct(cand, mult, PEER_TOPK + 1)
                best = top[0]
                c16 = c17 = top[0]
                seen = jnp.zeros_like(best)
                for j in range(PEER_TOPK + 1):
                    c16 = jnp.where(seen < PEER_TOPK, top[j], c16)
                    c17 = jnp.where(seen < PEER_TOPK + 1, top[j], c17)
                    seen = seen + ntop[j]
                runner_up = jnp.maximum(c17, jnp.maximum(v1[PEER_TOPK] + v2[0], v1[0] + v2[PEER_TOPK]))
                thr = 0.5 * (c16 + runner_up)
                zsum = jnp.sum(jnp.where(cand >= thr, mult * jnp.exp(cand - best), 0.0), axis=0, keepdims=True)
                p_s[h, :, ls] = jnp.exp(s2t - v2[0])
                c_s[h, :, ls] = jnp.exp(s1t - v1[0]) / zsum
                th_s[h, :, ls] = thr - s1t
            return 0

        lax.fori_loop(0, PEER_HEADS, head_body, 0)

    act = jnp.dot(u_ref[...], xnt_s[...], preferred_element_type=F32)
    for j in range(PEER_EB):
        e1 = eb * PEER_EB + j
        a = act[j * PEER_KEYS:(j + 1) * PEER_KEYS]
        gate = jnp.zeros((PEER_KEYS, tm), F32)
        for h in range(PEER_HEADS):
            th = th_s[h, pl.ds(e1, 1), :]
            ch = c_s[h, pl.ds(e1, 1), :]
            gate = gate + jnp.where(s2_s[h] >= th, p_s[h], 0.0) * ch
        w_s[j * PEER_KEYS:(j + 1) * PEER_KEYS, :] = (gate * jax.nn.gelu(a)).astype(BF16)
    acc_s[...] += jnp.dot(vt_ref[0], w_s[...], preferred_element_type=F32)

    @pl.when(eb == pl.num_programs(1) - 1)
    def _epilogue():
        o_ref[...] = x_ref[...] + acc_s[...].T


def _peer(x, g_ffn, wq_t, keys1, keys2, u_bf, vt_bf, *, tm):
    T = x.shape[0]
    nq = PEER_HEADS * 2 * PEER_HALF
    eblk = PEER_EB * PEER_KEYS
    x_spec = pl.BlockSpec((tm, D_MODEL), lambda i, e: (i, 0))
    sc = lambda: pltpu.VMEM((PEER_HEADS, PEER_KEYS, tm), F32)
    return pl.pallas_call(
        functools.partial(_peer_kernel, tm=tm),
        grid=(T // tm, PEER_KEYS // PEER_EB),
        in_specs=[x_spec, _full((1, D_MODEL)), _full((nq, D_MODEL)),
                  _full((PEER_HEADS, PEER_KEYS, PEER_HALF)), _full((PEER_HEADS, PEER_KEYS, PEER_HALF)),
                  pl.BlockSpec((eblk, D_MODEL), lambda i, e: (e, 0)),
                  pl.BlockSpec((1, D_MODEL, eblk), lambda i, e: (e, 0, 0))],
        out_specs=x_spec,
        out_shape=jax.ShapeDtypeStruct((T, D_MODEL), F32),
        scratch_shapes=[pltpu.VMEM((D_MODEL, tm), BF16), pltpu.VMEM((nq, tm), F32),
                        sc(), sc(), sc(), sc(),
                        pltpu.VMEM((D_MODEL, tm), F32), pltpu.VMEM((eblk, tm), BF16)],
        compiler_params=_params("parallel", "arbitrary"),
        name="peer",
    )(x, g_ffn.reshape(1, -1), wq_t, keys1, keys2, u_bf, vt_bf)


def _ple_kernel(x_ref, pe_ref, gple_ref, wg_ref, bg_ref, wp_ref, gfin_ref, o_ref):
    x = x_ref[...]
    gate = jax.nn.sigmoid(_mm(_rms(x, gple_ref[...]), wg_ref[...], False) + bg_ref[...])
    x = x + gate * _mm(pe_ref[...], wp_ref[...], False)
    o_ref[...] = _rms(x, gfin_ref[...])


def _ple(x, pe, g_ple, w_gate, b_gate, w_ple, g_final, *, tm):
    T = x.shape[0]
    row = lambda n: pl.BlockSpec((tm, n), lambda i: (i, 0))
    return pl.pallas_call(
        _ple_kernel,
        grid=(T // tm,),
        in_specs=[row(D_MODEL), row(PLE_DIM), _full((1, D_MODEL)), _full((D_MODEL, D_MODEL)),
                  _full((1, D_MODEL)), _full((PLE_DIM, D_MODEL)), _full((1, D_MODEL))],
        out_specs=row(D_MODEL),
        out_shape=jax.ShapeDtypeStruct((T, D_MODEL), F32),
        compiler_params=_params("parallel"),
        name="ple",
    )(x, pe, g_ple.reshape(1, -1), w_gate.astype(BF16), b_gate.reshape(1, -1), w_ple.astype(BF16),
      g_final.reshape(1, -1))


def _pad_rows(a, B, L, Lp):
    return jnp.pad(a.reshape(B, L, -1), ((0, 0), (0, Lp - L), (0, 0))).reshape(B * Lp, -1)


def _layer(x, pe, h0r, h0i, s0, wts, *, B, L, sample):
    (g_mix, w_pad, wgk_pad, b_gk, ab_re, ab_im, bmat, cre, cim, d_skip, w_glu, b_glu, g_ssm_out,
     g_gla_head, w_out, g_ffn, wq_t, keys1, keys2, u_bf, vt_bf, g_ple, w_ple, w_ple_gate, b_ple_gate,
     g_final) = wts
    T = B * L
    tm = min(512, T)
    wdt = F32 if sample else BF16
    u, q, k, v, og, lg = _inproj(x, g_mix, w_pad.astype(wdt), wgk_pad.astype(wdt), b_gk, tm=tm, exact=sample)

    if sample:
        ssm_out, hr, hi = _s5(u, h0r, h0i, ab_re, ab_im, bmat, cre, cim, d_skip, w_glu, b_glu, g_ssm_out,
                              bb=32, tl=L, cw=256, exact=True)
    else:
        ssm_out, hr, hi = _s5(u.reshape(B, L, D_SSM), h0r, h0i, ab_re, ab_im, bmat, cre, cim, d_skip,
                              w_glu, b_glu, g_ssm_out, bb=B, tl=64, cw=512, exact=False)
        ssm_out = ssm_out.reshape(T, D_SSM)

    s0t = s0.transpose(0, 1, 3, 2)
    if sample:
        Lp = 8
        pad = lambda a: _pad_rows(a, B, L, Lp)
        gla_out, st = _gla(pad(q), pad(k), pad(v), pad(lg), pad(og), s0t, g_gla_head,
                           B=B, L=Lp, nb=8, nchunk=1, c=Lp, sub=Lp)
        gla_out = gla_out.reshape(B, Lp, D_GLA)[:, :L].reshape(T, D_GLA)
    else:
        gla_out, st = _gla(q, k, v, lg, og, s0t, g_gla_head,
                           B=B, L=L, nb=1, nchunk=4, c=GLA_CHUNK, sub=GLA_SUB)
    s_gla = st.transpose(0, 1, 3, 2)

    x = _outproj(x, ssm_out, gla_out, w_out, tm=tm)
    x = _peer(x, g_ffn, wq_t, keys1, keys2, u_bf, vt_bf, tm=min(256, T))
    y = _ple(x, pe, g_ple, w_ple_gate, b_ple_gate, w_ple, g_final, tm=tm)
    return y, hr, hi, s_gla


def kernel(x_prompt, x_sample, state_ssm_re, state_ssm_im, state_gla, p_prompt, p_sample, g_mix, w_in, w_out, lam_re, lam_im, log_step, b_re, b_im, c_re, c_im, d_skip, w_glu, b_glu, g_ssm_out, w_gk2, b_gk, g_gla_head, g_ffn, w_query, sub_keys1, sub_keys2, peer_u, peer_v, g_ple, w_ple, w_ple_gate, b_ple_gate, g_final):
    assert w_in.shape[0] == 1, "single layer"
    bp, lp, _ = x_prompt.shape
    bs, ls, _ = x_sample.shape
    G, P = N_GROUPS, SSM_STATE

    ab_re, ab_im, bp_re, bp_im = _s5_prep(lam_re[0], lam_im[0], log_step[0], b_re[0], b_im[0])
    bmat = jnp.concatenate([_block_diag(bp_re), _block_diag(bp_im)], axis=1)
    cre = _block_diag(c_re[0].transpose(0, 2, 1))
    cim = _block_diag(c_im[0].transpose(0, 2, 1))
    wi = w_in[0]
    o_gk = D_SSM + 2 * GLA_QK + D_GLA
    w_pad = jnp.concatenate(
        [wi[:, :o_gk], wi[:, o_gk + GLA_RANK:], wi[:, o_gk:o_gk + GLA_RANK],
         jnp.zeros((D_MODEL, LANE - GLA_RANK), F32)], axis=1)
    wgk_pad = jnp.concatenate([w_gk2[0], jnp.zeros((LANE - GLA_RANK, GLA_QK), F32)], axis=0)
    wq_t = w_query[0].T.astype(BF16)
    u_bf = peer_u[0].astype(BF16)
    nblk = PEER_KEYS // PEER_EB
    vt_bf = peer_v[0].astype(BF16).reshape(nblk, PEER_EB * PEER_KEYS, D_MODEL).transpose(0, 2, 1)
    wts = (g_mix[0], w_pad, wgk_pad, b_gk[0], ab_re.reshape(-1), ab_im.reshape(-1), bmat, cre, cim,
           d_skip[0], w_glu[0], b_glu[0], g_ssm_out[0], g_gla_head[0], w_out[0], g_ffn[0], wq_t,
           sub_keys1[0], sub_keys2[0], u_bf, vt_bf, g_ple[0], w_ple[0], w_ple_gate[0], b_ple_gate[0], g_final)

    zs = jnp.zeros((bp, SSM_LANES), F32)
    zg = jnp.zeros((bp, GLA_HEADS, GLA_DK, GLA_DV), F32)
    yp, hrp, hip, sgp = _layer(x_prompt.reshape(bp * lp, D_MODEL), p_prompt[0].reshape(bp * lp, PLE_DIM),
                               zs, zs, zg, wts, B=bp, L=lp, sample=False)
    ys, hrs, his, sgs = _layer(x_sample.reshape(bs * ls, D_MODEL), p_sample[0].reshape(bs * ls, PLE_DIM),
                               state_ssm_re[0].reshape(bs, SSM_LANES), state_ssm_im[0].reshape(bs, SSM_LANES),
                               state_gla[0], wts, B=bs, L=ls, sample=True)
    st = lambda a, b: a.reshape(1, b, G, P)
    return (yp.reshape(bp, lp, D_MODEL), ys.reshape(bs, ls, D_MODEL),
            st(hrp, bp), st(hip, bp), sgp[None],
            st(hrs, bs), st(his, bs), sgs[None])
```
